```python
import math
import jax
import jax.numpy as jnp
from jax import lax
import numpy as np


D_MODEL = 1024
BATCH = 2
SEQ = 16384
DEPTH = 4

D_MIX = D_MODEL
HEAD_DIM = 64
N_Q_HEADS = (D_MIX // 2) // HEAD_DIM
N_KV_HEADS = 2
Q_PER_KV = N_Q_HEADS // N_KV_HEADS
D_ATTN = N_Q_HEADS * HEAD_DIM
D_KV = N_KV_HEADS * HEAD_DIM
WINDOW = 128
ATTN_BLOCK = 128
ROPE_THETA = 10000.0
D_S5 = D_MIX // 4
S5_GROUP = 16
S5_GROUPS = D_S5 // S5_GROUP
S5_STATE = 64
D_LRU = D_MIX - D_ATTN - D_S5
LRU_HEADS = 4
LRU_HEAD_DIM = D_LRU // LRU_HEADS
LRU_CONV = 4
LRU_C = 8.0
D_IN = D_ATTN + 2 * D_KV + D_S5 + 2 * D_LRU
SPLITS = (D_ATTN, D_ATTN + D_KV, D_ATTN + 2 * D_KV, D_ATTN + 2 * D_KV + D_S5, D_ATTN + 2 * D_KV + D_S5 + D_LRU)
D_FF = ((8 * D_MODEL // 3 + 127) // 128) * 128
FFN_CONV = 3
ALPHA = (2 * DEPTH) ** 0.25
BETA = (8 * DEPTH) ** -0.25
LN_EPS = 1e-5
RMS_EPS = 1e-6

kernel_name = 'hymba_swa_s5_rglru_deepnorm_trunk'


def layer_norm(x, g, b):
    xf = x.astype(jnp.float32)
    mu = jnp.mean(xf, axis=-1, keepdims=True)
    xc = xf - mu
    var = jnp.mean(jnp.square(xc), axis=-1, keepdims=True)
    y = xc * lax.rsqrt(var + LN_EPS) * g.astype(jnp.float32) + b.astype(jnp.float32)
    return y.astype(x.dtype)


def group_rmsnorm(parts, g):
    normed = [p.astype(jnp.float32) * lax.rsqrt(jnp.mean(jnp.square(p.astype(jnp.float32)), axis=-1, keepdims=True) + RMS_EPS) for p in parts]
    return (jnp.concatenate(normed, axis=-1) * g.astype(jnp.float32)).astype(parts[0].dtype)


def causal_dwconv(x, w):
    K = w.shape[0]
    L = x.shape[1]
    xp = jnp.pad(x, ((0, 0), (K - 1, 0), (0, 0)))
    y = xp[:, 0:L] * w[0]
    for k in range(1, K):
        y = y + xp[:, k:k + L] * w[k]
    return y


def rope_tables(L, dtype):
    inv_freq = ROPE_THETA ** (-jnp.arange(0, HEAD_DIM, 2, dtype=jnp.float32) / HEAD_DIM)
    ang = jnp.arange(L, dtype=jnp.float32)[:, None] * inv_freq[None, :]
    return jnp.cos(ang).astype(dtype)[None, :, None, :], jnp.sin(ang).astype(dtype)[None, :, None, :]


def apply_rope(t, cos, sin):
    t1, t2 = jnp.split(t, 2, axis=-1)
    return jnp.concatenate([t1 * cos - t2 * sin, t2 * cos + t1 * sin], axis=-1)


def sliding_window_attention(q, k, v, sinks):
    Bsz, L, _, _ = q.shape
    nb = L // ATTN_BLOCK
    qb = q.reshape(Bsz, nb, ATTN_BLOCK, N_KV_HEADS, Q_PER_KV, HEAD_DIM)

    def band(t):
        tp = jnp.pad(t, ((0, 0), (ATTN_BLOCK, 0), (0, 0), (0, 0))).reshape(Bsz, nb + 1, ATTN_BLOCK, N_KV_HEADS, HEAD_DIM)
        return jnp.concatenate([tp[:, :-1], tp[:, 1:]], axis=2)

    kb, vb = band(k), band(v)
    scores = jnp.einsum('bnqkgd,bnskd->bnkgqs', qb, kb).astype(jnp.float32) * (HEAD_DIM ** -0.5)
    qi = jnp.arange(ATTN_BLOCK)[:, None]
    si = jnp.arange(2 * ATTN_BLOCK)[None, :]
    diff = qi + ATTN_BLOCK - si
    blk = jnp.arange(nb)[:, None, None]
    valid = (diff >= 0) & (diff < WINDOW) & (blk * ATTN_BLOCK + si[None] - ATTN_BLOCK >= 0)
    scores = jnp.where(valid[None, :, None, None], scores, -jnp.inf)
    sink = sinks.astype(jnp.float32).reshape(N_KV_HEADS, Q_PER_KV)[None, None, :, :, None, None]
    m = jnp.maximum(jnp.max(scores, axis=-1, keepdims=True), sink)
    p = jnp.exp(scores - m)
    denom = jnp.sum(p, axis=-1, keepdims=True) + jnp.exp(sink - m)
    p = (p / denom).astype(v.dtype)
    out = jnp.einsum('bnkgqs,bnskd->bnqkgd', p, vb)
    return out.reshape(Bsz, L, D_ATTN)


def _complex_affine_combine(e1, e2):
    ar1, ai1, br1, bi1 = e1
    ar2, ai2, br2, bi2 = e2
    ar = ar2 * ar1 - ai2 * ai1
    ai = ar2 * ai1 + ai2 * ar1
    br = ar2 * br1 - ai2 * bi1 + br2
    bi = ar2 * bi1 + ai2 * br1 + bi2
    return (ar, ai, br, bi)


def _real_affine_combine(e1, e2):
    a1, b1 = e1
    a2, b2 = e2
    return (a1 * a2, a2 * b1 + b2)


def s5_mixer(u, a_re, a_im, b_re, b_im, c_re, c_im, d, log_dt, glu_w, glu_b):
    f32 = jnp.float32
    Bsz, L, _ = u.shape
    uf = u.astype(f32).reshape(Bsz, L, S5_GROUPS, S5_GROUP)
    lam_re = jnp.minimum(a_re.astype(f32), -1e-4)
    lam_im = a_im.astype(f32)
    dt = jnp.exp(log_dt.astype(f32))[:, None]
    decay = jnp.exp(dt * lam_re)
    ang = dt * lam_im
    abar_re = decay * jnp.cos(ang)
    abar_im = decay * jnp.sin(ang)
    den = jnp.square(lam_re) + jnp.square(lam_im)
    nr = abar_re - 1.0
    ni = abar_im
    coef_re = (nr * lam_re + ni * lam_im) / den
    coef_im = (ni * lam_re - nr * lam_im) / den
    br = b_re.astype(f32)
    bi = b_im.astype(f32)
    bbar_re = coef_re[..., None] * br - coef_im[..., None] * bi
    bbar_im = coef_re[..., None] * bi + coef_im[..., None] * br
    bu_re = jnp.einsum('blgc,gpc->blgp', uf, bbar_re)
    bu_im = jnp.einsum('blgc,gpc->blgp', uf, bbar_im)
    shape = bu_re.shape
    elems = (jnp.broadcast_to(abar_re, shape), jnp.broadcast_to(abar_im, shape), bu_re, bu_im)
    _, _, h_re, h_im = lax.associative_scan(_complex_affine_combine, elems, axis=1)
    y = (jnp.einsum('blgp,gcp->blgc', h_re, c_re.astype(f32))
         - jnp.einsum('blgp,gcp->blgc', h_im, c_im.astype(f32))
         + d.astype(f32).reshape(S5_GROUPS, S5_GROUP) * uf)
    y = jax.nn.gelu(y.reshape(Bsz, L, D_S5))
    y = y * jax.nn.sigmoid(y @ glu_w.astype(f32) + glu_b.astype(f32))
    return y.astype(u.dtype)


def rg_lru_mixer(xr, gate, conv_w, conv_b, wx, bx, wa, ba, a_param):
    f32 = jnp.float32
    Bsz, L, _ = xr.shape
    xc = causal_dwconv(xr, conv_w) + conv_b
    xh = xc.reshape(Bsz, L, LRU_HEADS, LRU_HEAD_DIM)
    gx = jax.nn.sigmoid(jnp.einsum('blhi,hij->blhj', xh, wx).reshape(Bsz, L, D_LRU) + bx)
    ga = jax.nn.sigmoid(jnp.einsum('blhi,hij->blhj', xh, wa).reshape(Bsz, L, D_LRU) + ba)
    log_a = -LRU_C * ga.astype(f32) * jax.nn.softplus(-a_param.astype(f32))
    a = jnp.exp(log_a)
    mult = jnp.sqrt(-jnp.expm1(2.0 * log_a))
    mult = jnp.where((jnp.arange(L) == 0)[None, :, None], 1.0, mult)
    b = mult * gx.astype(f32) * xc.astype(f32)
    _, h = lax.associative_scan(_real_affine_combine, (a, b), axis=1)
    return h.astype(xr.dtype) * jax.nn.gelu(gate)


def conv_gated_mlp(x, w_gate, w_up, conv_w, conv_b, w_down):
    g = causal_dwconv(x @ w_gate, conv_w) + conv_b
    return (jax.nn.silu(g) * (x @ w_up)) @ w_down


def setup_inputs(seed: int = 0) -> dict:
    key = jax.random.key(seed)
    ks = jax.random.split(key, 40)
    f32 = jnp.float32

    def nrm(k, shape, scale):
        return scale * jax.random.normal(k, shape, f32)

    n_idx = jnp.arange(S5_STATE, dtype=f32)
    a0 = jax.random.uniform(ks[20], (DEPTH, D_LRU), f32, 0.9, 0.999)
    return {
        'x': nrm(ks[0], (BATCH, SEQ, D_MODEL), 1.0),
        'w_in': nrm(ks[1], (DEPTH, D_MODEL, D_IN), D_MODEL ** -0.5),
        'b_in': nrm(ks[2], (DEPTH, D_IN), 0.01),
        'attn_sinks': nrm(ks[3], (DEPTH, N_Q_HEADS), 0.5),
        's5_a_re': -0.5 + nrm(ks[4], (DEPTH, S5_GROUPS, S5_STATE), 0.01),
        's5_a_im': jnp.pi * n_idx + nrm(ks[5], (DEPTH, S5_GROUPS, S5_STATE), 0.01),
        's5_b_re': nrm(ks[6], (DEPTH, S5_GROUPS, S5_STATE, S5_GROUP), (2 * S5_GROUP) ** -0.5),
        's5_b_im': nrm(ks[7], (DEPTH, S5_GROUPS, S5_STATE, S5_GROUP), (2 * S5_GROUP) ** -0.5),
        's5_c_re': nrm(ks[8], (DEPTH, S5_GROUPS, S5_GROUP, S5_STATE), (2 * S5_STATE) ** -0.5),
        's5_c_im': nrm(ks[9], (DEPTH, S5_GROUPS, S5_GROUP, S5_STATE), (2 * S5_STATE) ** -0.5),
        's5_d': nrm(ks[10], (DEPTH, D_S5), 1.0),
        's5_log_dt': jax.random.uniform(ks[11], (DEPTH, S5_GROUPS), f32, math.log(1e-3), math.log(1e-1)),
        's5_glu_w': nrm(ks[12], (DEPTH, D_S5, D_S5), D_S5 ** -0.5),
        's5_glu_b': nrm(ks[13], (DEPTH, D_S5), 0.01),
        'lru_conv_w': nrm(ks[14], (DEPTH, LRU_CONV, D_LRU), LRU_CONV ** -0.5),
        'lru_conv_b': nrm(ks[15], (DEPTH, D_LRU), 0.01),
        'lru_wx': nrm(ks[16], (DEPTH, LRU_HEADS, LRU_HEAD_DIM, LRU_HEAD_DIM), LRU_HEAD_DIM ** -0.5),
        'lru_bx': nrm(ks[17], (DEPTH, D_LRU), 0.01),
        'lru_wa': nrm(ks[18], (DEPTH, LRU_HEADS, LRU_HEAD_DIM, LRU_HEAD_DIM), LRU_HEAD_DIM ** -0.5),
        'lru_ba': nrm(ks[19], (DEPTH, D_LRU), 0.01),
        'lru_a_param': jnp.log(a0) - jnp.log1p(-a0),
        'mix_norm_g': 1.0 + nrm(ks[21], (DEPTH, D_MIX), 0.01),
        'w_out': nrm(ks[22], (DEPTH, D_MIX, D_MODEL), BETA * D_MIX ** -0.5),
        'b_out': nrm(ks[23], (DEPTH, D_MODEL), 0.01),
        'ln1_g': 1.0 + nrm(ks[24], (DEPTH, D_MODEL), 0.01),
        'ln1_b': nrm(ks[25], (DEPTH, D_MODEL), 0.01),
        'ffn_w_gate': nrm(ks[26], (DEPTH, D_MODEL, D_FF), D_MODEL ** -0.5),
        'ffn_w_up': nrm(ks[27], (DEPTH, D_MODEL, D_FF), D_MODEL ** -0.5),
        'ffn_conv_w': nrm(ks[28], (DEPTH, FFN_CONV, D_FF), FFN_CONV ** -0.5),
        'ffn_conv_b': nrm(ks[29], (DEPTH, D_FF), 0.01),
        'ffn_w_down': nrm(ks[30], (DEPTH, D_FF, D_MODEL), BETA * D_FF ** -0.5),
        'ln2_g': 1.0 + nrm(ks[31], (DEPTH, D_MODEL), 0.01),
        'ln2_b': nrm(ks[32], (DEPTH, D_MODEL), 0.01),
    }


def reference(x, w_in, b_in, attn_sinks, s5_a_re, s5_a_im, s5_b_re, s5_b_im, s5_c_re, s5_c_im,
              s5_d, s5_log_dt, s5_glu_w, s5_glu_b, lru_conv_w, lru_conv_b, lru_wx, lru_bx, lru_wa,
              lru_ba, lru_a_param, mix_norm_g, w_out, b_out, ln1_g, ln1_b, ffn_w_gate, ffn_w_up,
              ffn_conv_w, ffn_conv_b, ffn_w_down, ln2_g, ln2_b):
    Bsz, L, _ = x.shape
    cos, sin = rope_tables(L, x.dtype)
    for l in range(DEPTH):
        proj = x @ w_in[l] + b_in[l]
        q, k, v, u, xr, gate = jnp.split(proj, SPLITS, axis=-1)
        q = apply_rope(q.reshape(Bsz, L, N_Q_HEADS, HEAD_DIM), cos, sin)
        k = apply_rope(k.reshape(Bsz, L, N_KV_HEADS, HEAD_DIM), cos, sin)
        v = v.reshape(Bsz, L, N_KV_HEADS, HEAD_DIM)
        y_attn = sliding_window_attention(q, k, v, attn_sinks[l])
        y_s5 = s5_mixer(u, s5_a_re[l], s5_a_im[l], s5_b_re[l], s5_b_im[l], s5_c_re[l], s5_c_im[l],
                        s5_d[l], s5_log_dt[l], s5_glu_w[l], s5_glu_b[l])
        y_lru = rg_lru_mixer(xr, gate, lru_conv_w[l], lru_conv_b[l], lru_wx[l], lru_bx[l],
                             lru_wa[l], lru_ba[l], lru_a_param[l])
        mix = group_rmsnorm((y_attn, y_s5, y_lru), mix_norm_g[l])
        x = layer_norm(ALPHA * x + mix @ w_out[l] + b_out[l], ln1_g[l], ln1_b[l])
        f = conv_gated_mlp(x, ffn_w_gate[l], ffn_w_up[l], ffn_conv_w[l], ffn_conv_b[l], ffn_w_down[l])
        x = layer_norm(ALPHA * x + f, ln2_g[l], ln2_b[l])
    return x
```

```python
import functools
import math

import jax
import jax.numpy as jnp
from jax import lax
from jax.experimental import pallas as pl
from jax.experimental.pallas import tpu as pltpu

F32 = jnp.float32
BF16 = jnp.bfloat16

D_MODEL = 1024
HEAD_DIM = 64
N_Q_HEADS = 8
N_KV_HEADS = 2
D_ATTN = N_Q_HEADS * HEAD_DIM
D_KV = N_KV_HEADS * HEAD_DIM
WINDOW = 128
ROPE_THETA = 10000.0
D_S5 = 256
S5_GROUP = 16
S5_GROUPS = 16
S5_STATE = 64
D_LRU = 256
LRU_HEADS = 4
LRU_CONV = 4
LRU_C = 8.0
D_IN = D_ATTN + 2 * D_KV + D_S5 + 2 * D_LRU
D_FF = 2816
FFN_CONV = 3
DEPTH = 4
ALPHA = (2 * DEPTH) ** 0.25
LN_EPS = 1e-5
RMS_EPS = 1e-6

OFF_K = D_ATTN
OFF_V = OFF_K + D_KV
OFF_U = OFF_V + D_KV
OFF_XR = OFF_U + D_S5
OFF_GATE = OFF_XR + D_LRU

LANES = 128
SUBLANES = 8
S5_CHUNK = LANES
NEG_BIG = -1e30
VMEM_LIMIT = 56 * 1024 * 1024
FF_CHUNKS = ((0, 768), (768, 768), (1536, 768), (2304, 512))


def _const_spec(shape):
    nd = len(shape)
    return pl.BlockSpec(shape, lambda *_: (0,) * nd, pipeline_mode=pl.Buffered(1))


def _shift_rows(x, d, prev8):
    r = pltpu.roll(x, d, 0)
    rp = pltpu.roll(prev8, d, 0)
    row = lax.broadcasted_iota(jnp.int32, prev8.shape, 0)
    head = jnp.where(row < d, rp, r[0:SUBLANES])
    return jnp.concatenate([head, r[SUBLANES:]], axis=0)


def _layer_norm(x, g, b):
    mu = jnp.mean(x, axis=-1, keepdims=True)
    xc = x - mu
    var = jnp.mean(xc * xc, axis=-1, keepdims=True)
    return xc * lax.rsqrt(var + LN_EPS) * g + b


def _rms_norm(x, g):
    return x * lax.rsqrt(jnp.mean(x * x, axis=-1, keepdims=True) + RMS_EPS) * g


def _mixer_front_kernel(sinks_ref, x_ref, cos_ref, sin_ref, w_in_ref, b_in_ref,
                        convw_ref, convb_ref, wx_ref, bx_ref, wa_ref, ba_ref, sp_ref,
                        g_attn_ref, g_lru_ref,
                        attn_o, lru_o, u_o, ut_o,
                        q_buf, kv_buf, attn_buf, xr_buf, h_carry, *, ts):
    i = pl.program_id(1)

    xb = x_ref[...].astype(BF16)
    proj = jnp.dot(xb, w_in_ref[...], preferred_element_type=F32) + b_in_ref[...]

    cos = cos_ref[...]
    sin = sin_ref[...]
    lane = lax.broadcasted_iota(jnp.int32, (ts, LANES), 1)
    first_half = (lane % HEAD_DIM) < (HEAD_DIM // 2)
    low_head = lane < HEAD_DIM

    def rope(t):
        partner = jnp.where(first_half, pltpu.roll(t, LANES - HEAD_DIM // 2, 1),
                            pltpu.roll(t, HEAD_DIM // 2, 1))
        return t * cos + partner * sin

    scale = HEAD_DIM ** -0.5
    for pair in range(N_Q_HEADS // 2):
        t = proj[:, pair * LANES:(pair + 1) * LANES]
        q_buf[:, pair * LANES:(pair + 1) * LANES] = (rope(t) * scale).astype(BF16)

    @pl.when(i == 0)
    def _():
        kv_buf[:, 0:WINDOW, :] = jnp.zeros((8, WINDOW, LANES), BF16)
        xr_buf[0:SUBLANES, :] = jnp.zeros((SUBLANES, D_LRU), F32)
        h_carry[...] = jnp.zeros((1, D_LRU), F32)

    @pl.when(i > 0)
    def _():
        kv_buf[:, 0:WINDOW, :] = kv_buf[:, ts:ts + WINDOW, :]
        xr_buf[0:SUBLANES, :] = xr_buf[ts:ts + SUBLANES, :]

    kr = rope(proj[:, OFF_K:OFF_K + D_KV])
    vv = proj[:, OFF_V:OFF_V + D_KV]
    zero = jnp.zeros_like(kr)
    for base, t in ((0, kr), (4, vv)):
        sw = pltpu.roll(t, HEAD_DIM, 1)
        kv_buf[base + 0, WINDOW:WINDOW + ts, :] = jnp.where(low_head, t, zero).astype(BF16)
        kv_buf[base + 1, WINDOW:WINDOW + ts, :] = jnp.where(low_head, zero, sw).astype(BF16)
        kv_buf[base + 2, WINDOW:WINDOW + ts, :] = jnp.where(low_head, sw, zero).astype(BF16)
        kv_buf[base + 3, WINDOW:WINDOW + ts, :] = jnp.where(low_head, zero, t).astype(BF16)

    qi = lax.broadcasted_iota(jnp.int32, (WINDOW, 2 * WINDOW), 0)
    si = lax.broadcasted_iota(jnp.int32, (WINDOW, 2 * WINDOW), 1)
    rel = si - qi
    band = jnp.logical_and(rel >= 1, rel <= WINDOW)
    bias_norm = jnp.where(band, 0.0, NEG_BIG).astype(F32)
    bias_first = jnp.where(jnp.logical_and(band, si >= WINDOW), 0.0, NEG_BIG).astype(F32)

    def attn_block(j):
        r0 = j * WINDOW
        bias = jnp.where(i == 0, bias_first, bias_norm) if j == 0 else bias_norm
        for pair in range(N_Q_HEADS // 2):
            kvh = pair // 2
            qp = q_buf[pl.ds(r0, WINDOW), pair * LANES:(pair + 1) * LANES]
            acc = None
            for half in range(2):
                head = 2 * pair + half
                kk = kv_buf[2 * kvh + half, pl.ds(r0, 2 * WINDOW), :]
                s = lax.dot_general(qp, kk, (((1,), (1,)), ((), ())),
                                    preferred_element_type=F32) + bias
                sink = sinks_ref[head]
                m = jnp.maximum(jnp.max(s, axis=1, keepdims=True), sink)
                p = jnp.exp(s - m)
                denom = jnp.sum(p, axis=1, keepdims=True) + jnp.exp(sink - m)
                vblk = kv_buf[4 + 2 * kvh + half, pl.ds(r0, 2 * WINDOW), :]
                o = jnp.dot(p.astype(BF16), vblk, preferred_element_type=F32)
                o = o * (1.0 / denom)
                acc = o if acc is None else acc + o
            attn_buf[pl.ds(r0, WINDOW), pair * LANES:(pair + 1) * LANES] = acc

    for j in range(ts // WINDOW):
        attn_block(j)
    attn_o[...] = _rms_norm(attn_buf[...], g_attn_ref[...]).astype(BF16)

    u = proj[:, OFF_U:OFF_U + D_S5]
    u_o[...] = u
    ut_o[...] = u.T.astype(BF16)

    xr_buf[SUBLANES:SUBLANES + ts, :] = proj[:, OFF_XR:OFF_XR + D_LRU]
    xc = convb_ref[...]
    for k in range(LRU_CONV):
        off = SUBLANES - (LRU_CONV - 1) + k
        xc = xc + convw_ref[k:k + 1, :] * xr_buf[off:off + ts, :]
    xcb = xc.astype(BF16)
    gx = jax.nn.sigmoid(jnp.dot(xcb, wx_ref[...], preferred_element_type=F32) + bx_ref[...])
    ga = jax.nn.sigmoid(jnp.dot(xcb, wa_ref[...], preferred_element_type=F32) + ba_ref[...])
    log_a = (-LRU_C) * ga * sp_ref[...]
    a = jnp.exp(log_a)
    mult = jnp.sqrt(1.0 - a * a)
    row = lax.broadcasted_iota(jnp.int32, (ts, D_LRU), 0)
    mult = jnp.where(jnp.logical_and(i == 0, row == 0), 1.0, mult)
    b = mult * gx * xc
    d = 1
    while d < ts:
        keep = row >= d
        b = b + jnp.where(keep, a * pltpu.roll(b, d, 0), 0.0)
        a = jnp.where(keep, a * pltpu.roll(a, d, 0), a)
        d *= 2
    h = b + a * h_carry[...]
    h_carry[...] = h[ts - 1:ts, :]
    gate = proj[:, OFF_GATE:OFF_GATE + D_LRU]
    y_lru = h * jax.nn.gelu(gate)
    lru_o[...] = _rms_norm(y_lru, g_lru_ref[...]).astype(BF16)


def _mixer_front(x2d, cos_t, sin_t, sinks, w_in, b_in, convw, convb, wx, bx, wa, ba, sp,
                 g_attn, g_lru, *, batch, seq, ts):
    n = batch * seq
    nt = seq // ts
    tok = lambda b, i: (b * nt + i, 0)
    in_specs = [
        pl.BlockSpec(memory_space=pltpu.SMEM),
        pl.BlockSpec((ts, D_MODEL), tok),
        pl.BlockSpec((ts, LANES), lambda b, i: (i, 0)),
        pl.BlockSpec((ts, LANES), lambda b, i: (i, 0)),
        _const_spec((D_MODEL, D_IN)),
        _const_spec((1, D_IN)),
        _const_spec((LRU_CONV, D_LRU)),
        _const_spec((1, D_LRU)),
        _const_spec((D_LRU, D_LRU)),
        _const_spec((1, D_LRU)),
        _const_spec((D_LRU, D_LRU)),
        _const_spec((1, D_LRU)),
        _const_spec((1, D_LRU)),
        _const_spec((1, D_ATTN)),
        _const_spec((1, D_LRU)),
    ]
    out_shape = (
        jax.ShapeDtypeStruct((n, D_ATTN), BF16),
        jax.ShapeDtypeStruct((n, D_LRU), BF16),
        jax.ShapeDtypeStruct((n, D_S5), F32),
        jax.ShapeDtypeStruct((D_S5, n), BF16),
    )
    out_specs = (
        pl.BlockSpec((ts, D_ATTN), tok),
        pl.BlockSpec((ts, D_LRU), tok),
        pl.BlockSpec((ts, D_S5), tok),
        pl.BlockSpec((D_S5, ts), lambda b, i: (0, b * nt + i)),
    )
    scratch = [
        pltpu.VMEM((ts, D_ATTN), BF16),
        pltpu.VMEM((8, ts + WINDOW, LANES), BF16),
        pltpu.VMEM((ts, D_ATTN), F32),
        pltpu.VMEM((ts + SUBLANES, D_LRU), F32),
        pltpu.VMEM((1, D_LRU), F32),
    ]
    return pl.pallas_call(
        functools.partial(_mixer_front_kernel, ts=ts),
        out_shape=out_shape,
        grid=(batch, nt),
        in_specs=in_specs,
        out_specs=out_specs,
        scratch_shapes=scratch,
        compiler_params=pltpu.CompilerParams(
            dimension_semantics=("arbitrary", "arbitrary"), vmem_limit_bytes=VMEM_LIMIT),
        name="mixer_front",
    )(sinks, x2d, cos_t, sin_t, w_in, b_in, convw, convb, wx, bx, wa, ba, sp, g_attn, g_lru)


def _s5_core_kernel(ut_ref, toep_ref, p_ref, q_ref, a1_ref, a2_ref, y_ref, *, chunks_per_seq):
    rows = ut_ref.shape[1]
    width = S5_GROUP * S5_CHUNK
    u = jnp.concatenate([ut_ref[c] for c in range(S5_GROUP)], axis=1)
    s = jnp.dot(u, p_ref[0], preferred_element_type=F32)
    rowc = lax.broadcasted_iota(jnp.int32, (rows, 2 * S5_STATE), 0) % chunks_per_seq
    h = s
    k = 0
    while (1 << k) < chunks_per_seq:
        d = 1 << k
        hs = jnp.where(rowc >= d, pltpu.roll(h, d, 0), 0.0)
        h = h + a1_ref[0, k:k + 1, :] * hs + a2_ref[0, k:k + 1, :] * pltpu.roll(hs, S5_STATE, 1)
        k += 1
    h_prev = jnp.where(rowc >= 1, pltpu.roll(h, 1, 0), 0.0).astype(BF16)
    blk = 2 * LANES
    for j in range(width // blk):
        y = (jnp.dot(u, toep_ref[0, :, j * blk:(j + 1) * blk], preferred_element_type=F32)
             + jnp.dot(h_prev, q_ref[0, :, j * blk:(j + 1) * blk], preferred_element_type=F32))
        y_ref[2 * j] = y[:, :LANES]
        y_ref[2 * j + 1] = y[:, LANES:]


def _s5_core(ut3, toep, p, q, a1, a2, *, chunks_per_seq):
    rows = ut3.shape[1]
    width = S5_GROUP * S5_CHUNK
    nk = a1.shape[1]
    return pl.pallas_call(
        functools.partial(_s5_core_kernel, chunks_per_seq=chunks_per_seq),
        out_shape=jax.ShapeDtypeStruct((D_S5, rows, S5_CHUNK), F32),
        grid=(S5_GROUPS,),
        in_specs=[
            pl.BlockSpec((S5_GROUP, rows, S5_CHUNK), lambda g: (g, 0, 0)),
            pl.BlockSpec((1, width, width), lambda g: (g, 0, 0)),
            pl.BlockSpec((1, width, 2 * S5_STATE), lambda g: (g, 0, 0)),
            pl.BlockSpec((1, 2 * S5_STATE, width), lambda g: (g, 0, 0)),
            pl.BlockSpec((1, nk, 2 * S5_STATE), lambda g: (g, 0, 0)),
            pl.BlockSpec((1, nk, 2 * S5_STATE), lambda g: (g, 0, 0)),
        ],
        out_specs=pl.BlockSpec((S5_GROUP, rows, S5_CHUNK), lambda g: (g, 0, 0)),
        compiler_params=pltpu.CompilerParams(
            dimension_semantics=("arbitrary",), vmem_limit_bytes=VMEM_LIMIT),
        name="s5_core",
    )(ut3, toep, p, q, a1, a2)


def _s5_tables(a_re, a_im, b_re, b_im, c_re, c_im, log_dt, chunks_per_seq):
    hi = lax.Precision.HIGHEST
    t_len = S5_CHUNK
    lam_re = jnp.minimum(a_re, -1e-4)
    lam_im = a_im
    dt = jnp.exp(log_dt)[:, None]
    decay = jnp.exp(dt * lam_re)
    ang = dt * lam_im
    ab_re = decay * jnp.cos(ang)
    ab_im = decay * jnp.sin(ang)
    den = jnp.square(lam_re) + jnp.square(lam_im)
    nr = ab_re - 1.0
    ni = ab_im
    coef_re = (nr * lam_re + ni * lam_im) / den
    coef_im = (ni * lam_re - nr * lam_im) / den
    bb_re = coef_re[..., None] * b_re - coef_im[..., None] * b_im
    bb_im = coef_re[..., None] * b_im + coef_im[..., None] * b_re
    pw_re = jnp.ones((1,) + ab_re.shape, F32)
    pw_im = jnp.zeros((1,) + ab_re.shape, F32)
    cur_re, cur_im = ab_re, ab_im
    while pw_re.shape[0] < t_len:
        n_re = pw_re * cur_re - pw_im * cur_im
        n_im = pw_re * cur_im + pw_im * cur_re
        pw_re = jnp.concatenate([pw_re, n_re], axis=0)
        pw_im = jnp.concatenate([pw_im, n_im], axis=0)
        cur_re, cur_im = cur_re * cur_re - cur_im * cur_im, 2.0 * cur_re * cur_im
    cp_re = c_re[None] * pw_re[:, :, None, :] - c_im[None] * pw_im[:, :, None, :]
    cp_im = c_re[None] * pw_im[:, :, None, :] + c_im[None] * pw_re[:, :, None, :]
    kern = (jnp.einsum('kgop,gpc->kgoc', cp_re, bb_re, precision=hi)
            - jnp.einsum('kgop,gpc->kgoc', cp_im, bb_im, precision=hi))
    kern = jnp.transpose(kern, (1, 3, 2, 0)).astype(BF16)
    g_, c_, o_ = kern.shape[:3]
    padded = jnp.concatenate([jnp.zeros((g_, c_, o_, t_len - 1), BF16), kern,
                              jnp.zeros((g_, c_, o_, 1), BF16)], axis=-1)
    tiled = jnp.tile(padded, (1, 1, 1, t_len))[..., :t_len * (2 * t_len - 1)]
    toep = tiled.reshape(g_, c_, o_, t_len, 2 * t_len - 1)[..., t_len - 1:]
    toep = jnp.transpose(toep, (0, 1, 3, 2, 4)).reshape(g_, c_ * t_len, o_ * t_len)
    rev_re = pw_re[::-1]
    rev_im = pw_im[::-1]
    p_re = rev_re[:, :, :, None] * bb_re[None] - rev_im[:, :, :, None] * bb_im[None]
    p_im = rev_re[:, :, :, None] * bb_im[None] + rev_im[:, :, :, None] * bb_re[None]
    p = jnp.concatenate([p_re, p_im], axis=2)
    p = jnp.transpose(p, (1, 3, 0, 2)).reshape(g_, c_ * t_len, 2 * S5_STATE).astype(BF16)
    p1_re = jnp.concatenate([pw_re[1:], cur_re[None]], axis=0)
    p1_im = jnp.concatenate([pw_im[1:], cur_im[None]], axis=0)
    q_re = c_re[None] * p1_re[:, :, None, :] - c_im[None] * p1_im[:, :, None, :]
    q_im = c_re[None] * p1_im[:, :, None, :] + c_im[None] * p1_re[:, :, None, :]
    q = jnp.concatenate([q_re, -q_im], axis=3)
    q = jnp.transpose(q, (1, 3, 2, 0)).reshape(g_, 2 * S5_STATE, o_ * t_len).astype(BF16)
    a1, a2 = [], []
    k = 0
    while (1 << k) < max(chunks_per_seq, 2):
        a1.append(jnp.concatenate([cur_re, cur_re], axis=-1))
        a2.append(jnp.concatenate([-cur_im, cur_im], axis=-1))
        cur_re, cur_im = cur_re * cur_re - cur_im * cur_im, 2.0 * cur_re * cur_im
        k += 1
    return toep, p, q, jnp.stack(a1, axis=1), jnp.stack(a2, axis=1)


def _post_ffn_kernel(x_ref, attn_ref, lru_ref, u_ref, yt_ref, d_ref, gluw_ref, glub_ref, g_s5_ref,
                     wout_ref, bout_ref, ln1g_ref, ln1b_ref, wg_ref, wu_ref, cw_ref, cb_ref, wd_ref,
                     ln2g_ref, ln2b_ref, o_ref, g_carry, *, ts):
    i = pl.program_id(1)

    @pl.when(i == 0)
    def _():
        g_carry[...] = jnp.zeros_like(g_carry)

    y = yt_ref[...].T + d_ref[...] * u_ref[...]
    y = jax.nn.gelu(y)
    z = y * jax.nn.sigmoid(
        jnp.dot(y.astype(BF16), gluw_ref[...], preferred_element_type=F32) + glub_ref[...])
    zn = _rms_norm(z, g_s5_ref[...]).astype(BF16)

    o1 = D_ATTN
    o2 = D_ATTN + D_S5
    mix = (jnp.dot(attn_ref[...], wout_ref[0:o1, :], preferred_element_type=F32)
           + jnp.dot(zn, wout_ref[o1:o2, :], preferred_element_type=F32)
           + jnp.dot(lru_ref[...], wout_ref[o2:D_MODEL, :], preferred_element_type=F32)
           + bout_ref[...])
    x1 = _layer_norm(ALPHA * x_ref[...] + mix, ln1g_ref[...], ln1b_ref[...])

    xb = x1.astype(BF16)
    acc = None
    for off, width in FF_CHUNKS:
        g = jnp.dot(xb, wg_ref[:, off:off + width], preferred_element_type=F32)
        prev8 = g_carry[:, off:off + width]
        g1 = _shift_rows(g, 1, prev8)
        g2 = _shift_rows(g, 2, prev8)
        g_carry[:, off:off + width] = g[ts - SUBLANES:ts, :]
        gc = (cw_ref[0:1, off:off + width] * g2 + cw_ref[1:2, off:off + width] * g1
              + cw_ref[2:3, off:off + width] * g + cb_ref[:, off:off + width])
        up = jnp.dot(xb, wu_ref[:, off:off + width], preferred_element_type=F32)
        act = (gc * jax.nn.sigmoid(gc) * up).astype(BF16)
        part = jnp.dot(act, wd_ref[off:off + width, :], preferred_element_type=F32)
        acc = part if acc is None else acc + part
    o_ref[...] = _layer_norm(ALPHA * x1 + acc, ln2g_ref[...], ln2b_ref[...])


def _post_ffn(x2d, attn_n, lru_n, u, yt, d, gluw, glub, g_s5, wout, bout, ln1g, ln1b,
              wg, wu, cw, cb, wd, ln2g, ln2b, *, batch, seq, ts):
    n = batch * seq
    nt = seq // ts
    tok = lambda b, i: (b * nt + i, 0)
    in_specs = [
        pl.BlockSpec((ts, D_MODEL), tok),
        pl.BlockSpec((ts, D_ATTN), tok),
        pl.BlockSpec((ts, D_LRU), tok),
        pl.BlockSpec((ts, D_S5), tok),
        pl.BlockSpec((D_S5, ts), lambda b, i: (0, b * nt + i)),
        _const_spec((1, D_S5)),
        _const_spec((D_S5, D_S5)),
        _const_spec((1, D_S5)),
        _const_spec((1, D_S5)),
        _const_spec((D_MODEL, D_MODEL)),
        _const_spec((1, D_MODEL)),
        _const_spec((1, D_MODEL)),
        _const_spec((1, D_MODEL)),
        _const_spec((D_MODEL, D_FF)),
        _const_spec((D_MODEL, D_FF)),
        _const_spec((FFN_CONV, D_FF)),
        _const_spec((1, D_FF)),
        _const_spec((D_FF, D_MODEL)),
        _const_spec((1, D_MODEL)),
        _const_spec((1, D_MODEL)),
    ]
    return pl.pallas_call(
        functools.partial(_post_ffn_kernel, ts=ts),
        out_shape=jax.ShapeDtypeStruct((n, D_MODEL), F32),
        grid=(batch, nt),
        in_specs=in_specs,
        out_specs=pl.BlockSpec((ts, D_MODEL), tok),
        scratch_shapes=[pltpu.VMEM((SUBLANES, D_FF), F32)],
        compiler_params=pltpu.CompilerParams(
            dimension_semantics=("arbitrary", "arbitrary"), vmem_limit_bytes=VMEM_LIMIT),
        name="post_ffn",
    )(x2d, attn_n, lru_n, u, yt, d, gluw, glub, g_s5, wout, bout, ln1g, ln1b,
      wg, wu, cw, cb, wd, ln2g, ln2b)


def _block_diag(w):
    h, d, _ = w.shape
    eye = jnp.eye(h, dtype=w.dtype)
    return (eye[:, None, :, None] * w[:, :, None, :]).reshape(h * d, h * d)


def _pick_tile(seq):
    for ts in (512, 256, 128):
        if seq % ts == 0:
            return ts
    raise ValueError("sequence length must be a multiple of 128")


def kernel(x, w_in, b_in, attn_sinks, s5_a_re, s5_a_im, s5_b_re, s5_b_im, s5_c_re, s5_c_im, s5_d, s5_log_dt, s5_glu_w, s5_glu_b, lru_conv_w, lru_conv_b, lru_wx, lru_bx, lru_wa, lru_ba, lru_a_param, mix_norm_g, w_out, b_out, ln1_g, ln1_b, ffn_w_gate, ffn_w_up, ffn_conv_w, ffn_conv_b, ffn_w_down, ln2_g, ln2_b):
    batch, seq, d_model = x.shape
    assert d_model == D_MODEL and seq % S5_CHUNK == 0
    depth = w_in.shape[0]
    ts = _pick_tile(seq)
    n = batch * seq
    chunks_per_seq = seq // S5_CHUNK
    assert chunks_per_seq & (chunks_per_seq - 1) == 0, "chunk count per sequence must be a power of two"

    inv_freq = ROPE_THETA ** (-jnp.arange(0, HEAD_DIM, 2, dtype=F32) / HEAD_DIM)
    ang = jnp.arange(seq, dtype=F32)[:, None] * inv_freq[None, :]
    cos_h = jnp.cos(ang)
    sin_h = jnp.sin(ang)
    cos_t = jnp.tile(jnp.concatenate([cos_h, cos_h], axis=1), (1, LANES // HEAD_DIM))
    sin_t = jnp.tile(jnp.concatenate([-sin_h, sin_h], axis=1), (1, LANES // HEAD_DIM))

    row = lambda v: v.reshape(1, -1).astype(F32)
    xf = x.reshape(n, D_MODEL)
    for l in range(depth):
        toep, p, q, a1, a2 = _s5_tables(s5_a_re[l], s5_a_im[l], s5_b_re[l], s5_b_im[l],
                                        s5_c_re[l], s5_c_im[l], s5_log_dt[l], chunks_per_seq)
        g_mix = mix_norm_g[l].astype(F32)
        attn_n, lru_n, u, ut = _mixer_front(
            xf, cos_t, sin_t, attn_sinks[l].astype(F32),
            w_in[l].astype(BF16), row(b_in[l]),
            lru_conv_w[l].astype(F32), row(lru_conv_b[l]),
            _block_diag(lru_wx[l]).astype(BF16), row(lru_bx[l]),
            _block_diag(lru_wa[l]).astype(BF16), row(lru_ba[l]),
            row(jax.nn.softplus(-lru_a_param[l].astype(F32))),
            row(g_mix[:D_ATTN]), row(g_mix[D_ATTN + D_S5:]),
            batch=batch, seq=seq, ts=ts)
        yt = _s5_core(ut.reshape(D_S5, n // S5_CHUNK, S5_CHUNK), toep, p, q, a1, a2,
                      chunks_per_seq=chunks_per_seq)
        xf = _post_ffn(
            xf, attn_n, lru_n, u, yt.reshape(D_S5, n),
            row(s5_d[l]), s5_glu_w[l].astype(BF16), row(s5_glu_b[l]),
            row(g_mix[D_ATTN:D_ATTN + D_S5]),
            w_out[l].astype(BF16), row(b_out[l]), row(ln1_g[l]), row(ln1_b[l]),
            ffn_w_gate[l].astype(BF16), ffn_w_up[l].astype(BF16),
            ffn_conv_w[l].astype(F32), row(ffn_conv_b[l]),
            ffn_w_down[l].astype(BF16), row(ln2_g[l]), row(ln2_b[l]),
            batch=batch, seq=seq, ts=ts)
    return xf.reshape(batch, seq, D_MODEL)
```

```python
import functools
import math

import jax
import jax.numpy as jnp
from jax import lax
from jax.experimental import pallas as pl
from jax.experimental.pallas import tpu as pltpu

F32 = jnp.float32
BF16 = jnp.bfloat16

D_MODEL = 1024
HEAD_DIM = 64
N_Q_HEADS = 8
N_KV_HEADS = 2
D_ATTN = N_Q_HEADS * HEAD_DIM
D_KV = N_KV_HEADS * HEAD_DIM
WINDOW = 128
ROPE_THETA = 10000.0
D_S5 = 256
S5_GROUP = 16
S5_GROUPS = 16
S5_STATE = 64
D_LRU = 256
LRU_HEADS = 4
LRU_CONV = 4
LRU_C = 8.0
D_IN = D_ATTN + 2 * D_KV + D_S5 + 2 * D_LRU
D_FF = 2816
FFN_CONV = 3
DEPTH = 4
ALPHA = (2 * DEPTH) ** 0.25
LN_EPS = 1e-5
RMS_EPS = 1e-6

OFF_K = D_ATTN
OFF_V = OFF_K + D_KV
OFF_U = OFF_V + D_KV
OFF_XR = OFF_U + D_S5
OFF_GATE = OFF_XR + D_LRU

LANES = 128
SUBLANES = 8
S5_CHUNK = LANES
NEG_BIG = -1e30
VMEM_LIMIT = 56 * 1024 * 1024
FF_CHUNKS = ((0, 768), (768, 768), (1536, 768), (2304, 512))


def _const_spec(shape):
    nd = len(shape)
    return pl.BlockSpec(shape, lambda *_: (0,) * nd, pipeline_mode=pl.Buffered(1))


def _shift_rows(x, d, prev8):
    r = pltpu.roll(x, d, 0)
    rp = pltpu.roll(prev8, d, 0)
    row = lax.broadcasted_iota(jnp.int32, prev8.shape, 0)
    head = jnp.where(row < d, rp, r[0:SUBLANES])
    return jnp.concatenate([head, r[SUBLANES:]], axis=0)


def _layer_norm(x, g, b):
    mu = jnp.mean(x, axis=-1, keepdims=True)
    xc = x - mu
    var = jnp.mean(xc * xc, axis=-1, keepdims=True)
    return xc * lax.rsqrt(var + LN_EPS) * g + b


def _rms_norm(x, g):
    return x * lax.rsqrt(jnp.mean(x * x, axis=-1, keepdims=True) + RMS_EPS) * g


def _mixer_front_kernel(sinks_ref, x_ref, cos_ref, sin_ref, w_in_ref, b_in_ref,
                        convw_ref, convb_ref, wx_ref, bx_ref, wa_ref, ba_ref, sp_ref,
                        g_attn_ref, g_lru_ref,
                        attn_o, lru_o, u_o, ut_o,
                        q_buf, kv_buf, attn_buf, xr_buf, h_carry, *, ts):
    i = pl.program_id(1)

    xb = x_ref[...].astype(BF16)
    proj = jnp.dot(xb, w_in_ref[...], preferred_element_type=F32) + b_in_ref[...]

    cos = cos_ref[...]
    sin = sin_ref[...]
    lane = lax.broadcasted_iota(jnp.int32, (ts, LANES), 1)
    first_half = (lane % HEAD_DIM) < (HEAD_DIM // 2)
    low_head = lane < HEAD_DIM

    def rope(t):
        partner = jnp.where(first_half, pltpu.roll(t, LANES - HEAD_DIM // 2, 1),
                            pltpu.roll(t, HEAD_DIM // 2, 1))
        return t * cos + partner * sin

    scale = HEAD_DIM ** -0.5
    for pair in range(N_Q_HEADS // 2):
        t = proj[:, pair * LANES:(pair + 1) * LANES]
        q_buf[:, pair * LANES:(pair + 1) * LANES] = (rope(t) * scale).astype(BF16)

    @pl.when(i == 0)
    def _():
        kv_buf[:, 0:WINDOW, :] = jnp.zeros((8, WINDOW, LANES), BF16)
        xr_buf[0:SUBLANES, :] = jnp.zeros((SUBLANES, D_LRU), F32)
        h_carry[...] = jnp.zeros((1, D_LRU), F32)

    @pl.when(i > 0)
    def _():
        kv_buf[:, 0:WINDOW, :] = kv_buf[:, ts:ts + WINDOW, :]
        xr_buf[0:SUBLANES, :] = xr_buf[ts:ts + SUBLANES, :]

    kr = rope(proj[:, OFF_K:OFF_K + D_KV])
    vv = proj[:, OFF_V:OFF_V + D_KV]
    zero = jnp.zeros_like(kr)
    for base, t in ((0, kr), (4, vv)):
        sw = pltpu.roll(t, HEAD_DIM, 1)
        kv_buf[base + 0, WINDOW:WINDOW + ts, :] = jnp.where(low_head, t, zero).astype(BF16)
        kv_buf[base + 1, WINDOW:WINDOW + ts, :] = jnp.where(low_head, zero, sw).astype(BF16)
        kv_buf[base + 2, WINDOW:WINDOW + ts, :] = jnp.where(low_head, sw, zero).astype(BF16)
        kv_buf[base + 3, WINDOW:WINDOW + ts, :] = jnp.where(low_head, zero, t).astype(BF16)

    qi = lax.broadcasted_iota(jnp.int32, (WINDOW, 2 * WINDOW), 0)
    si = lax.broadcasted_iota(jnp.int32, (WINDOW, 2 * WINDOW), 1)
    rel = si - qi
    band = jnp.logical_and(rel >= 1, rel <= WINDOW)
    bias_norm = jnp.where(band, 0.0, NEG_BIG).astype(F32)
    bias_first = jnp.where(jnp.logical_and(band, si >= WINDOW), 0.0, NEG_BIG).astype(F32)

    def attn_block(j):
        r0 = j * WINDOW
        bias = jnp.where(i == 0, bias_first, bias_norm) if j == 0 else bias_norm
        for pair in range(N_Q_HEADS // 2):
            kvh = pair // 2
            qp = q_buf[pl.ds(r0, WINDOW), pair * LANES:(pair + 1) * LANES]
            acc = None
            for half in range(2):
                head = 2 * pair + half
                kk = kv_buf[2 * kvh + half, pl.ds(r0, 2 * WINDOW), :]
                s = lax.dot_general(qp, kk, (((1,), (1,)), ((), ())),
                                    preferred_element_type=F32) + bias
                sink = sinks_ref[head]
                m = jnp.maximum(jnp.max(s, axis=1, keepdims=True), sink)
                p = jnp.exp(s - m)
                denom = jnp.sum(p, axis=1, keepdims=True) + jnp.exp(sink - m)
                vblk = kv_buf[4 + 2 * kvh + half, pl.ds(r0, 2 * WINDOW), :]
                o = jnp.dot(p.astype(BF16), vblk, preferred_element_type=F32)
                o = o * (1.0 / denom)
                acc = o if acc is None else acc + o
            attn_buf[pl.ds(r0, WINDOW), pair * LANES:(pair + 1) * LANES] = acc

    for j in range(ts // WINDOW):
        attn_block(j)
    attn_o[...] = _rms_norm(attn_buf[...], g_attn_ref[...]).astype(BF16)

    u = proj[:, OFF_U:OFF_U + D_S5]
    u_o[...] = u
    ut_o[...] = u.T.astype(BF16)

    xr_buf[SUBLANES:SUBLANES + ts, :] = proj[:, OFF_XR:OFF_XR + D_LRU]
    xc = convb_ref[...]
    for k in range(LRU_CONV):
        off = SUBLANES - (LRU_CONV - 1) + k
        xc = xc + convw_ref[k:k + 1, :] * xr_buf[off:off + ts, :]
    xcb = xc.astype(BF16)
    gx = jax.nn.sigmoid(jnp.dot(xcb, wx_ref[...], preferred_element_type=F32) + bx_ref[...])
    ga = jax.nn.sigmoid(jnp.dot(xcb, wa_ref[...], preferred_element_type=F32) + ba_ref[...])
    log_a = (-LRU_C) * ga * sp_ref[...]
    a = jnp.exp(log_a)
    mult = jnp.sqrt(1.0 - a * a)
    row = lax.broadcasted_iota(jnp.int32, (ts, D_LRU), 0)
    mult = jnp.where(jnp.logical_and(i == 0, row == 0), 1.0, mult)
    b = mult * gx * xc
    d = 1
    while d < ts:
        keep = row >= d
        b = b + jnp.where(keep, a * pltpu.roll(b, d, 0), 0.0)
        a = jnp.where(keep, a * pltpu.roll(a, d, 0), a)
        d *= 2
    h = b + a * h_carry[...]
    h_carry[...] = h[ts - 1:ts, :]
    gate = proj[:, OFF_GATE:OFF_GATE + D_LRU]
    y_lru = h * jax.nn.gelu(gate)
    lru_o[...] = _rms_norm(y_lru, g_lru_ref[...]).astype(BF16)


def _mixer_front(x2d, cos_t, sin_t, sinks, w_in, b_in, convw, convb, wx, bx, wa, ba, sp,
                 g_attn, g_lru, *, batch, seq, ts):
    n = batch * seq
    nt = seq // ts
    tok = lambda b, i: (b * nt + i, 0)
    in_specs = [
        pl.BlockSpec(memory_space=pltpu.SMEM),
        pl.BlockSpec((ts, D_MODEL), tok),
        pl.BlockSpec((ts, LANES), lambda b, i: (i, 0)),
        pl.BlockSpec((ts, LANES), lambda b, i: (i, 0)),
        _const_spec((D_MODEL, D_IN)),
        _const_spec((1, D_IN)),
        _const_spec((LRU_CONV, D_LRU)),
        _const_spec((1, D_LRU)),
        _const_spec((D_LRU, D_LRU)),
        _const_spec((1, D_LRU)),
        _const_spec((D_LRU, D_LRU)),
        _const_spec((1, D_LRU)),
        _const_spec((1, D_LRU)),
        _const_spec((1, D_ATTN)),
        _const_spec((1, D_LRU)),
    ]
    out_shape = (
        jax.ShapeDtypeStruct((n, D_ATTN), BF16),
        jax.ShapeDtypeStruct((n, D_LRU), BF16),
        jax.ShapeDtypeStruct((n, D_S5), F32),
        jax.ShapeDtypeStruct((D_S5, n), BF16),
    )
    out_specs = (
        pl.BlockSpec((ts, D_ATTN), tok),
        pl.BlockSpec((ts, D_LRU), tok),
        pl.BlockSpec((ts, D_S5), tok),
        pl.BlockSpec((D_S5, ts), lambda b, i: (0, b * nt + i)),
    )
    scratch = [
        pltpu.VMEM((ts, D_ATTN), BF16),
        pltpu.VMEM((8, ts + WINDOW, LANES), BF16),
        pltpu.VMEM((ts, D_ATTN), F32),
        pltpu.VMEM((ts + SUBLANES, D_LRU), F32),
        pltpu.VMEM((1, D_LRU), F32),
    ]
    return pl.pallas_call(
        functools.partial(_mixer_front_kernel, ts=ts),
        out_shape=out_shape,
        grid=(batch, nt),
        in_specs=in_specs,
        out_specs=out_specs,
        scratch_shapes=scratch,
        compiler_params=pltpu.CompilerParams(
            dimension_semantics=("arbitrary", "arbitrary"), vmem_limit_bytes=VMEM_LIMIT),
        name="mixer_front",
    )(sinks, x2d, cos_t, sin_t, w_in, b_in, convw, convb, wx, bx, wa, ba, sp, g_attn, g_lru)


def _s5_core_kernel(ut_ref, kc_ref, p_ref, q_ref, a1_ref, a2_ref, y_ref, tbuf, *, chunks_per_seq):
    rows = ut_ref.shape[1]
    u = jnp.concatenate([ut_ref[c] for c in range(S5_GROUP)], axis=1)
    s = jnp.dot(u, p_ref[0], preferred_element_type=F32)
    rowc = lax.broadcasted_iota(jnp.int32, (rows, 2 * S5_STATE), 0) % chunks_per_seq
    h = s
    k = 0
    while (1 << k) < chunks_per_seq:
        d = 1 << k
        hs = jnp.where(rowc >= d, pltpu.roll(h, d, 0), 0.0)
        h = h + a1_ref[0, k:k + 1, :] * hs + a2_ref[0, k:k + 1, :] * pltpu.roll(hs, S5_STATE, 1)
        k += 1
    h_prev = jnp.where(rowc >= 1, pltpu.roll(h, 1, 0), 0.0).astype(BF16)
    s_i = lax.broadcasted_iota(jnp.int32, (S5_CHUNK, S5_CHUNK), 0)
    t_i = lax.broadcasted_iota(jnp.int32, (S5_CHUNK, S5_CHUNK), 1)
    causal = t_i >= s_i
    blk = 2 * LANES
    for j in range(S5_GROUP // 2):
        slab = kc_ref[0, 2 * S5_GROUP * j:2 * S5_GROUP * (j + 1), :]
        for oo in range(2):
            for c in range(S5_GROUP):
                r = oo * S5_GROUP + c
                lag = jnp.broadcast_to(slab[r:r + 1, :], (S5_CHUNK, S5_CHUNK))
                tz = pltpu.roll(lag, 0, 1, stride=1, stride_axis=0)
                tbuf[j % 2, c * S5_CHUNK:(c + 1) * S5_CHUNK, oo * LANES:(oo + 1) * LANES] = (
                    jnp.where(causal, tz, 0.0).astype(BF16))
        y = (jnp.dot(u, tbuf[j % 2], preferred_element_type=F32)
             + jnp.dot(h_prev, q_ref[0, :, j * blk:(j + 1) * blk], preferred_element_type=F32))
        y_ref[2 * j] = y[:, :LANES]
        y_ref[2 * j + 1] = y[:, LANES:]


def _s5_core(ut3, kc, p, q, a1, a2, *, chunks_per_seq):
    rows = ut3.shape[1]
    width = S5_GROUP * S5_CHUNK
    nk = a1.shape[1]
    return pl.pallas_call(
        functools.partial(_s5_core_kernel, chunks_per_seq=chunks_per_seq),
        out_shape=jax.ShapeDtypeStruct((D_S5, rows, S5_CHUNK), F32),
        grid=(S5_GROUPS,),
        in_specs=[
            pl.BlockSpec((S5_GROUP, rows, S5_CHUNK), lambda g: (g, 0, 0)),
            pl.BlockSpec((1, S5_GROUP * S5_GROUP, S5_CHUNK), lambda g: (g, 0, 0)),
            pl.BlockSpec((1, width, 2 * S5_STATE), lambda g: (g, 0, 0)),
            pl.BlockSpec((1, 2 * S5_STATE, width), lambda g: (g, 0, 0)),
            pl.BlockSpec((1, nk, 2 * S5_STATE), lambda g: (g, 0, 0)),
            pl.BlockSpec((1, nk, 2 * S5_STATE), lambda g: (g, 0, 0)),
        ],
        out_specs=pl.BlockSpec((S5_GROUP, rows, S5_CHUNK), lambda g: (g, 0, 0)),
        scratch_shapes=[pltpu.VMEM((2, width, 2 * LANES), BF16)],
        compiler_params=pltpu.CompilerParams(
            dimension_semantics=("arbitrary",), vmem_limit_bytes=VMEM_LIMIT),
        name="s5_core",
    )(ut3, kc, p, q, a1, a2)


def _s5_tables(a_re, a_im, b_re, b_im, c_re, c_im, log_dt, chunks_per_seq):
    hi = lax.Precision.HIGHEST
    t_len = S5_CHUNK
    lam_re = jnp.minimum(a_re, -1e-4)
    lam_im = a_im
    dt = jnp.exp(log_dt)[:, None]
    decay = jnp.exp(dt * lam_re)
    ang = dt * lam_im
    ab_re = decay * jnp.cos(ang)
    ab_im = decay * jnp.sin(ang)
    den = jnp.square(lam_re) + jnp.square(lam_im)
    nr = ab_re - 1.0
    ni = ab_im
    coef_re = (nr * lam_re + ni * lam_im) / den
    coef_im = (ni * lam_re - nr * lam_im) / den
    bb_re = coef_re[..., None] * b_re - coef_im[..., None] * b_im
    bb_im = coef_re[..., None] * b_im + coef_im[..., None] * b_re
    pw_re = jnp.ones((1,) + ab_re.shape, F32)
    pw_im = jnp.zeros((1,) + ab_re.shape, F32)
    cur_re, cur_im = ab_re, ab_im
    while pw_re.shape[0] < t_len:
        n_re = pw_re * cur_re - pw_im * cur_im
        n_im = pw_re * cur_im + pw_im * cur_re
        pw_re = jnp.concatenate([pw_re, n_re], axis=0)
        pw_im = jnp.concatenate([pw_im, n_im], axis=0)
        cur_re, cur_im = cur_re * cur_re - cur_im * cur_im, 2.0 * cur_re * cur_im
    cp_re = c_re[None] * pw_re[:, :, None, :] - c_im[None] * pw_im[:, :, None, :]
    cp_im = c_re[None] * pw_im[:, :, None, :] + c_im[None] * pw_re[:, :, None, :]
    kern = (jnp.einsum('kgop,gpc->kgoc', cp_re, bb_re, precision=hi)
            - jnp.einsum('kgop,gpc->kgoc', cp_im, bb_im, precision=hi))
    g_, c_ = bb_re.shape[0], bb_re.shape[2]
    o_ = c_re.shape[1]
    kern = jnp.transpose(kern, (1, 2, 3, 0)).reshape(g_, o_ * c_, t_len)
    rev_re = pw_re[::-1]
    rev_im = pw_im[::-1]
    p_re = rev_re[:, :, :, None] * bb_re[None] - rev_im[:, :, :, None] * bb_im[None]
    p_im = rev_re[:, :, :, None] * bb_im[None] + rev_im[:, :, :, None] * bb_re[None]
    p = jnp.concatenate([p_re, p_im], axis=2)
    p = jnp.transpose(p, (1, 3, 0, 2)).reshape(g_, c_ * t_len, 2 * S5_STATE).astype(BF16)
    p1_re = jnp.concatenate([pw_re[1:], cur_re[None]], axis=0)
    p1_im = jnp.concatenate([pw_im[1:], cur_im[None]], axis=0)
    q_re = c_re[None] * p1_re[:, :, None, :] - c_im[None] * p1_im[:, :, None, :]
    q_im = c_re[None] * p1_im[:, :, None, :] + c_im[None] * p1_re[:, :, None, :]
    q = jnp.concatenate([q_re, -q_im], axis=3)
    q = jnp.transpose(q, (1, 3, 2, 0)).reshape(g_, 2 * S5_STATE, o_ * t_len).astype(BF16)
    a1, a2 = [], []
    k = 0
    while (1 << k) < max(chunks_per_seq, 2):
        a1.append(jnp.concatenate([cur_re, cur_re], axis=-1))
        a2.append(jnp.concatenate([-cur_im, cur_im], axis=-1))
        cur_re, cur_im = cur_re * cur_re - cur_im * cur_im, 2.0 * cur_re * cur_im
        k += 1
    return kern, p, q, jnp.stack(a1, axis=1), jnp.stack(a2, axis=1)


def _post_ffn_kernel(x_ref, attn_ref, lru_ref, u_ref, yt_ref, d_ref, gluw_ref, glub_ref, g_s5_ref,
                     wout_ref, bout_ref, ln1g_ref, ln1b_ref, wg_ref, wu_ref, cw_ref, cb_ref, wd_ref,
                     ln2g_ref, ln2b_ref, o_ref, g_carry, *, ts):
    i = pl.program_id(1)

    @pl.when(i == 0)
    def _():
        g_carry[...] = jnp.zeros_like(g_carry)

    y = yt_ref[...].T + d_ref[...] * u_ref[...]
    y = jax.nn.gelu(y)
    z = y * jax.nn.sigmoid(
        jnp.dot(y.astype(BF16), gluw_ref[...], preferred_element_type=F32) + glub_ref[...])
    zn = _rms_norm(z, g_s5_ref[...]).astype(BF16)

    o1 = D_ATTN
    o2 = D_ATTN + D_S5
    mix = (jnp.dot(attn_ref[...], wout_ref[0:o1, :], preferred_element_type=F32)
           + jnp.dot(zn, wout_ref[o1:o2, :], preferred_element_type=F32)
           + jnp.dot(lru_ref[...], wout_ref[o2:D_MODEL, :], preferred_element_type=F32)
           + bout_ref[...])
    x1 = _layer_norm(ALPHA * x_ref[...] + mix, ln1g_ref[...], ln1b_ref[...])

    xb = x1.astype(BF16)
    acc = None
    for off, width in FF_CHUNKS:
        g = jnp.dot(xb, wg_ref[:, off:off + width], preferred_element_type=F32)
        prev8 = g_carry[:, off:off + width]
        g1 = _shift_rows(g, 1, prev8)
        g2 = _shift_rows(g, 2, prev8)
        g_carry[:, off:off + width] = g[ts - SUBLANES:ts, :]
        gc = (cw_ref[0:1, off:off + width] * g2 + cw_ref[1:2, off:off + width] * g1
              + cw_ref[2:3, off:off + width] * g + cb_ref[:, off:off + width])
        up = jnp.dot(xb, wu_ref[:, off:off + width], preferred_element_type=F32)
        act = (gc * jax.nn.sigmoid(gc) * up).astype(BF16)
        part = jnp.dot(act, wd_ref[off:off + width, :], preferred_element_type=F32)
        acc = part if acc is None else acc + part
    o_ref[...] = _layer_norm(ALPHA * x1 + acc, ln2g_ref[...], ln2b_ref[...])


def _post_ffn(x2d, attn_n, lru_n, u, yt, d, gluw, glub, g_s5, wout, bout, ln1g, ln1b,
              wg, wu, cw, cb, wd, ln2g, ln2b, *, batch, seq, ts):
    n = batch * seq
    nt = seq // ts
    tok = lambda b, i: (b * nt + i, 0)
    in_specs = [
        pl.BlockSpec((ts, D_MODEL), tok),
        pl.BlockSpec((ts, D_ATTN), tok),
        pl.BlockSpec((ts, D_LRU), tok),
        pl.BlockSpec((ts, D_S5), tok),
        pl.BlockSpec((D_S5, ts), lambda b, i: (0, b * nt + i)),
        _const_spec((1, D_S5)),
        _const_spec((D_S5, D_S5)),
        _const_spec((1, D_S5)),
        _const_spec((1, D_S5)),
        _const_spec((D_MODEL, D_MODEL)),
        _const_spec((1, D_MODEL)),
        _const_spec((1, D_MODEL)),
        _const_spec((1, D_MODEL)),
        _const_spec((D_MODEL, D_FF)),
        _const_spec((D_MODEL, D_FF)),
        _const_spec((FFN_CONV, D_FF)),
        _const_spec((1, D_FF)),
        _const_spec((D_FF, D_MODEL)),
        _const_spec((1, D_MODEL)),
        _const_spec((1, D_MODEL)),
    ]
    return pl.pallas_call(
        functools.partial(_post_ffn_kernel, ts=ts),
        out_shape=jax.ShapeDtypeStruct((n, D_MODEL), F32),
        grid=(batch, nt),
        in_specs=in_specs,
        out_specs=pl.BlockSpec((ts, D_MODEL), tok),
        scratch_shapes=[pltpu.VMEM((SUBLANES, D_FF), F32)],
        compiler_params=pltpu.CompilerParams(
            dimension_semantics=("arbitrary", "arbitrary"), vmem_limit_bytes=VMEM_LIMIT),
        name="post_ffn",
    )(x2d, attn_n, lru_n, u, yt, d, gluw, glub, g_s5, wout, bout, ln1g, ln1b,
      wg, wu, cw, cb, wd, ln2g, ln2b)


def _block_diag(w):
    h, d, _ = w.shape
    eye = jnp.eye(h, dtype=w.dtype)
    return (eye[:, None, :, None] * w[:, :, None, :]).reshape(h * d, h * d)


def _pick_tile(seq):
    for ts in (512, 256, 128):
        if seq % ts == 0:
            return ts
    raise ValueError("sequence length must be a multiple of 128")


def kernel(x, w_in, b_in, attn_sinks, s5_a_re, s5_a_im, s5_b_re, s5_b_im, s5_c_re, s5_c_im, s5_d, s5_log_dt, s5_glu_w, s5_glu_b, lru_conv_w, lru_conv_b, lru_wx, lru_bx, lru_wa, lru_ba, lru_a_param, mix_norm_g, w_out, b_out, ln1_g, ln1_b, ffn_w_gate, ffn_w_up, ffn_conv_w, ffn_conv_b, ffn_w_down, ln2_g, ln2_b):
    batch, seq, d_model = x.shape
    assert d_model == D_MODEL and seq % S5_CHUNK == 0
    depth = w_in.shape[0]
    ts = _pick_tile(seq)
    n = batch * seq
    chunks_per_seq = seq // S5_CHUNK
    assert chunks_per_seq & (chunks_per_seq - 1) == 0, "chunk count per sequence must be a power of two"

    inv_freq = ROPE_THETA ** (-jnp.arange(0, HEAD_DIM, 2, dtype=F32) / HEAD_DIM)
    ang = jnp.arange(seq, dtype=F32)[:, None] * inv_freq[None, :]
    cos_h = jnp.cos(ang)
    sin_h = jnp.sin(ang)
    cos_t = jnp.tile(jnp.concatenate([cos_h, cos_h], axis=1), (1, LANES // HEAD_DIM))
    sin_t = jnp.tile(jnp.concatenate([-sin_h, sin_h], axis=1), (1, LANES // HEAD_DIM))

    row = lambda v: v.reshape(1, -1).astype(F32)
    xf = x.reshape(n, D_MODEL)
    for l in range(depth):
        kern, p, q, a1, a2 = _s5_tables(s5_a_re[l], s5_a_im[l], s5_b_re[l], s5_b_im[l],
                                        s5_c_re[l], s5_c_im[l], s5_log_dt[l], chunks_per_seq)
        g_mix = mix_norm_g[l].astype(F32)
        attn_n, lru_n, u, ut = _mixer_front(
            xf, cos_t, sin_t, attn_sinks[l].astype(F32),
            w_in[l].astype(BF16), row(b_in[l]),
            lru_conv_w[l].astype(F32), row(lru_conv_b[l]),
            _block_diag(lru_wx[l]).astype(BF16), row(lru_bx[l]),
            _block_diag(lru_wa[l]).astype(BF16), row(lru_ba[l]),
            row(jax.nn.softplus(-lru_a_param[l].astype(F32))),
            row(g_mix[:D_ATTN]), row(g_mix[D_ATTN + D_S5:]),
            batch=batch, seq=seq, ts=ts)
        yt = _s5_core(ut.reshape(D_S5, n // S5_CHUNK, S5_CHUNK), kern, p, q, a1, a2,
                      chunks_per_seq=chunks_per_seq)
        xf = _post_ffn(
            xf, attn_n, lru_n, u, yt.reshape(D_S5, n),
            row(s5_d[l]), s5_glu_w[l].astype(BF16), row(s5_glu_b[l]),
            row(g_mix[D_ATTN:D_ATTN + D_S5]),
            w_out[l].astype(BF16), row(b_out[l]), row(ln1_g[l]), row(ln1_b[l]),
            ffn_w_gate[l].astype(BF16), ffn_w_up[l].astype(BF16),
            ffn_conv_w[l].astype(F32), row(ffn_conv_b[l]),
            ffn_w_down[l].astype(BF16), row(ln2_g[l]), row(ln2_b[l]),
            batch=batch, seq=seq, ts=ts)
    return xf.reshape(batch, seq, D_MODEL)
```
